```python
import jax, jax.numpy as jnp
from jax import lax
import numpy as np

D_MODEL = 1024
BATCH = 4
SEQ = 4096
DEPTH = 4

CHUNK = 64
N_MEM = 256
EPS = 1e-6
HG_HEADS = 4
HG_DK = 128
HG_DV = 128
HG_W = HG_HEADS * HG_DK
CV_W = 512
CV_KERNEL = 31
POOL_WINDOWS = (2, 4, 8, 16)
POOL_GROUPS = 4
POOL_GW = 128
POOL_W = POOL_GROUPS * POOL_GW
LRU_W = 512
LRU_HEADS = 8
LRU_HD = LRU_W // LRU_HEADS
LRU_CONV = 4
LRU_C = 8.0
N_BRANCH = 4
BRANCH_W = 512
IN_SIZES = (HG_W, HG_W, HG_W, HG_W, 2 * CV_W, POOL_W, LRU_W, LRU_W, N_BRANCH * D_MODEL)
IN_W = HG_W * 4 + 2 * CV_W + POOL_W + 2 * LRU_W + N_BRANCH * D_MODEL
XA_HEADS = 4
XA_HD = D_MODEL // XA_HEADS
D_FF = 4 * D_MODEL

kernel_name = 'hybrid_chunk_causal_streaming_encoder'


def rmsnorm(x, w):
    xf = x.astype(jnp.float32)
    y = xf * lax.rsqrt(jnp.mean(xf * xf, axis=-1, keepdims=True) + EPS) * w.astype(jnp.float32)
    return y.astype(x.dtype)


def layernorm(x, w, b):
    xf = x.astype(jnp.float32)
    mu = jnp.mean(xf, axis=-1, keepdims=True)
    var = jnp.mean(jnp.square(xf - mu), axis=-1, keepdims=True)
    return ((xf - mu) * lax.rsqrt(var + EPS) * w.astype(jnp.float32) + b.astype(jnp.float32)).astype(x.dtype)


def causal_depthwise_conv(z, w, b):
    k = w.shape[0]
    out = lax.conv_general_dilated(z, w[:, None, :].astype(z.dtype), window_strides=(1,),
                                   padding=[(k - 1, 0)], dimension_numbers=('NWC', 'WIO', 'NWC'),
                                   feature_group_count=z.shape[-1])
    return out + b.astype(z.dtype)


def split_in(proj):
    parts = []
    off = 0
    for n in IN_SIZES:
        parts.append(proj[..., off:off + n])
        off += n
    return parts


def hgrn2_mixer(q, f_pre, v, g, lb, norm_w):
    bsz, seq, _ = q.shape
    n = seq // CHUNK
    f32 = jnp.float32
    fg = lb + (1.0 - lb) * jax.nn.sigmoid(f_pre.astype(f32))
    kk = 1.0 - fg
    logf = jnp.log(fg)
    qf = jax.nn.silu(q.astype(f32))

    def heads(t):
        return t.reshape(bsz, n, CHUNK, HG_HEADS, -1).transpose(1, 0, 3, 2, 4)

    qc, kc, vc, lc = heads(qf), heads(kk), heads(v.astype(f32)), heads(logf)
    bc = jnp.cumsum(lc, axis=3)
    causal = jnp.tril(jnp.ones((CHUNK, CHUNK), dtype=bool))

    def step(state, inp):
        q_, k_, v_, b_ = inp
        diff = b_[:, :, :, None, :] - b_[:, :, None, :, :]
        decay = jnp.exp(jnp.where(causal[:, :, None], diff, -jnp.inf))
        att = jnp.einsum('bhtd,bhsd,bhtsd->bhts', q_, k_, decay)
        o = (jnp.einsum('bhts,bhsv->bhtv', att, v_)
             + jnp.einsum('bhtd,bhdv->bhtv', q_ * jnp.exp(b_), state))
        b_last = b_[:, :, -1:, :]
        state = (jnp.exp(b_last[:, :, 0, :])[..., None] * state
                 + jnp.einsum('bhsd,bhsv->bhdv', k_ * jnp.exp(b_last - b_), v_))
        return state, o

    state0 = jnp.zeros((bsz, HG_HEADS, HG_DK, HG_DV), f32)
    _, o = lax.scan(step, state0, (qc, kc, vc, bc))
    o = o.transpose(1, 0, 3, 2, 4).reshape(bsz, seq, HG_HEADS, HG_DV)
    gh = g.astype(f32).reshape(bsz, seq, HG_HEADS, HG_DV)
    o = rmsnorm(o, norm_w) * jax.nn.silu(gh)
    return o.reshape(bsz, seq, HG_HEADS * HG_DV).astype(q.dtype)


def conformer_conv(u, dw_w, dw_b, ln_w, ln_b):
    a, b = u[..., :CV_W], u[..., CV_W:]
    z = a * jax.nn.sigmoid(b)
    z = causal_depthwise_conv(z, dw_w, dw_b)
    return jax.nn.silu(layernorm(z, ln_w, ln_b))


def pool_mixer(u, w_grp, scale):
    bsz, seq, _ = u.shape
    uf = u.astype(jnp.float32).reshape(bsz, seq, POOL_GROUPS, POOL_GW)
    cs = jnp.pad(jnp.cumsum(uf, axis=1), ((0, 0), (1, 0), (0, 0), (0, 0)))
    t = jnp.arange(seq)
    pooled = []
    for gi, w in enumerate(POOL_WINDOWS):
        lo = jnp.maximum(t + 1 - w, 0)
        csg = cs[:, :, gi]
        cnt = (t + 1 - lo).astype(jnp.float32)
        pooled.append((csg[:, 1:] - csg[:, lo]) / cnt[None, :, None])
    pooled = jnp.stack(pooled, axis=2) - uf
    y = jnp.einsum('bsgc,gcd->bsgd', pooled, w_grp.astype(jnp.float32))
    y = y.reshape(bsz, seq, POOL_W) * scale.astype(jnp.float32)
    return y.astype(u.dtype)


def rglru_mixer(xb, yb, conv_w, conv_b, w_a, b_a, w_x, b_x, lam):
    bsz, seq, _ = xb.shape
    f32 = jnp.float32
    xc = causal_depthwise_conv(xb, conv_w, conv_b)
    xh = xc.astype(f32).reshape(bsz, seq, LRU_HEADS, LRU_HD)
    r = jax.nn.sigmoid(jnp.einsum('bshi,hij->bshj', xh, w_a.astype(f32)) + b_a.astype(f32))
    ig = jax.nn.sigmoid(jnp.einsum('bshi,hij->bshj', xh, w_x.astype(f32)) + b_x.astype(f32))
    log_a = -LRU_C * r * jax.nn.softplus(-lam.astype(f32))
    a = jnp.exp(log_a)
    bterm = jnp.sqrt(-jnp.expm1(2.0 * log_a)) * (ig * xh)

    def combine(c1, c2):
        a1, b1 = c1
        a2, b2 = c2
        return a1 * a2, a2 * b1 + b2

    _, hstate = lax.associative_scan(combine, (a, bterm), axis=1)
    hstate = hstate.reshape(bsz, seq, LRU_W)
    return (hstate * jax.nn.gelu(yb.astype(f32))).astype(xb.dtype)


def mem_attention(h, mem_n, w_q, w_kv, w_o):
    bsz, seq, _ = h.shape
    q = (h @ w_q).reshape(bsz, seq, XA_HEADS, XA_HD)
    kv = mem_n @ w_kv
    k = kv[..., :D_MODEL].reshape(bsz, -1, XA_HEADS, XA_HD)
    v = kv[..., D_MODEL:].reshape(bsz, -1, XA_HEADS, XA_HD)
    s = jnp.einsum('bshd,bmhd->bhsm', q, k).astype(jnp.float32) * (XA_HD ** -0.5)
    p = jax.nn.softmax(s, axis=-1).astype(v.dtype)
    o = jnp.einsum('bhsm,bmhd->bshd', p, v).reshape(bsz, seq, D_MODEL)
    return o @ w_o


def setup_inputs(seed: int = 0) -> dict:
    key = jax.random.key(seed)
    ks = list(jax.random.split(key, 40))
    f32 = jnp.float32

    def nrm(shape, fan_in, scale=1.0):
        return jax.random.normal(ks.pop(), shape, f32) * (scale * fan_in ** -0.5)

    def gain(shape):
        return 1.0 + 0.02 * jax.random.normal(ks.pop(), shape, f32)

    def small(shape, s=0.02):
        return s * jax.random.normal(ks.pop(), shape, f32)

    u = jax.random.uniform(ks.pop(), (DEPTH, LRU_HEADS, LRU_HD), f32,
                           minval=-np.log(0.999) / LRU_C, maxval=-np.log(0.9) / LRU_C)
    lru_lambda = -jnp.log(jnp.expm1(u))
    return {
        'x': jax.random.normal(ks.pop(), (BATCH, SEQ, D_MODEL), f32),
        'mem': jax.random.normal(ks.pop(), (BATCH, N_MEM, D_MODEL), f32),
        'norm_mix_w': gain((DEPTH, D_MODEL)),
        'w_in': nrm((DEPTH, D_MODEL, IN_W), D_MODEL),
        'hg_lb_raw': small((DEPTH, HG_W), 0.5),
        'hg_norm_w': gain((DEPTH, HG_DV)),
        'cv_dw_w': nrm((DEPTH, CV_KERNEL, CV_W), CV_KERNEL),
        'cv_dw_b': small((DEPTH, CV_W)),
        'cv_ln_w': gain((DEPTH, CV_W)),
        'cv_ln_b': small((DEPTH, CV_W)),
        'pl_w': nrm((DEPTH, POOL_GROUPS, POOL_GW, POOL_GW), POOL_GW),
        'pl_scale': gain((DEPTH, POOL_W)),
        'lru_conv_w': nrm((DEPTH, LRU_CONV, LRU_W), LRU_CONV),
        'lru_conv_b': small((DEPTH, LRU_W)),
        'lru_wa': nrm((DEPTH, LRU_HEADS, LRU_HD, LRU_HD), LRU_HD),
        'lru_ba': small((DEPTH, LRU_HEADS, LRU_HD)),
        'lru_wx': nrm((DEPTH, LRU_HEADS, LRU_HD, LRU_HD), LRU_HD),
        'lru_bx': small((DEPTH, LRU_HEADS, LRU_HD)),
        'lru_lambda': lru_lambda,
        'gate_b': small((DEPTH, N_BRANCH, D_MODEL)),
        'w_branch': nrm((DEPTH, N_BRANCH, BRANCH_W, D_MODEL), BRANCH_W),
        'w_out': nrm((DEPTH, D_MODEL, D_MODEL), D_MODEL, 0.5),
        'norm_mem_w': gain((DEPTH, D_MODEL)),
        'mem_norm_w': gain((DEPTH, D_MODEL)),
        'xa_wq': nrm((DEPTH, D_MODEL, D_MODEL), D_MODEL),
        'xa_wkv': nrm((DEPTH, D_MODEL, 2 * D_MODEL), D_MODEL),
        'xa_wo': nrm((DEPTH, D_MODEL, D_MODEL), D_MODEL, 0.5),
        'norm_ffn_w': gain((DEPTH, D_MODEL)),
        'ffn_w1': nrm((DEPTH, D_MODEL, D_FF), D_MODEL),
        'ffn_w2': nrm((DEPTH, D_FF, D_MODEL), D_FF, 0.5),
        'final_norm_w': gain((D_MODEL,)),
    }


def reference(x, mem, norm_mix_w, w_in, hg_lb_raw, hg_norm_w, cv_dw_w, cv_dw_b, cv_ln_w, cv_ln_b,
              pl_w, pl_scale, lru_conv_w, lru_conv_b, lru_wa, lru_ba, lru_wx, lru_bx, lru_lambda,
              gate_b, w_branch, w_out, norm_mem_w, mem_norm_w, xa_wq, xa_wkv, xa_wo,
              norm_ffn_w, ffn_w1, ffn_w2, final_norm_w):
    bsz, seq, _ = x.shape
    lb = jnp.cumsum(jax.nn.softmax(hg_lb_raw.astype(jnp.float32), axis=0), axis=0)
    lb = lb - lb[0:1]
    for l in range(DEPTH):
        h = rmsnorm(x, norm_mix_w[l])
        proj = h @ w_in[l]
        hq, hf, hv, hg, cv_u, pl_u, lru_x, lru_y, gate_pre = split_in(proj)
        b_hg = hgrn2_mixer(hq, hf, hv, hg, lb[l], hg_norm_w[l])
        b_cv = conformer_conv(cv_u, cv_dw_w[l], cv_dw_b[l], cv_ln_w[l], cv_ln_b[l])
        b_pl = pool_mixer(pl_u, pl_w[l], pl_scale[l])
        b_lru = rglru_mixer(lru_x, lru_y, lru_conv_w[l], lru_conv_b[l], lru_wa[l], lru_ba[l],
                            lru_wx[l], lru_bx[l], lru_lambda[l])
        branches = jnp.stack([b_hg, b_cv, b_pl, b_lru], axis=2)
        up = jnp.einsum('bskm,kmd->bskd', branches, w_branch[l])
        gates = jax.nn.sigmoid(gate_pre.reshape(bsz, seq, N_BRANCH, D_MODEL) + gate_b[l])
        merged = jnp.sum(gates * up, axis=2)
        x = x + (merged @ w_out[l]).astype(x.dtype)
        h = rmsnorm(x, norm_mem_w[l])
        x = x + mem_attention(h, rmsnorm(mem, mem_norm_w[l]), xa_wq[l], xa_wkv[l], xa_wo[l]).astype(x.dtype)
        h = rmsnorm(x, norm_ffn_w[l])
        x = x + (jnp.square(jax.nn.relu(h @ ffn_w1[l])) @ ffn_w2[l]).astype(x.dtype)
    return rmsnorm(x, final_norm_w)
```

```python
import functools

import jax
import jax.numpy as jnp
from jax import lax
from jax.experimental import pallas as pl
from jax.experimental.pallas import tpu as pltpu

F32 = jnp.float32
BF16 = jnp.bfloat16

EPS = 1e-6
CHUNK = 64
SUB = 16
HG_HEADS = 4
HG_DK = 128
BRANCH_W = 512
N_BRANCH = 4
CV_KERNEL = 31
CV_HALO = 32
POOL_WINDOWS = (2, 4, 8, 16)
POOL_HALO = 16
LRU_CONV = 4
LRU_HALO = 8
LRU_C = 8.0
XA_HEADS = 4
MIX_IN_W = 9 * BRANCH_W

VMEM_LIMIT_BYTES = 56 * 1024 * 1024

SEQ_TILE = 256
ROW_TILE = 512


def _params(n_axes):
    return pltpu.CompilerParams(dimension_semantics=("arbitrary",) * n_axes,
                                vmem_limit_bytes=VMEM_LIMIT_BYTES)


def _rms(x, w):
    return x * lax.rsqrt(jnp.mean(x * x, axis=-1, keepdims=True) + EPS) * w


def _sigmoid(x):
    return 1.0 / (1.0 + jnp.exp(-x))


def _dot(a, b):
    return jnp.dot(a, b, preferred_element_type=F32)


def _dot_nt(a, b):
    return lax.dot_general(a, b, (((1,), (1,)), ((), ())), preferred_element_type=F32)


def _full(shape):
    return pl.BlockSpec(shape, lambda *_: (0,) * len(shape))


def _norm_matmul_kernel(x_ref, g_ref, w_ref, o_ref, *, col_chunk):
    h = _rms(x_ref[...], g_ref[...]).astype(BF16)
    n = w_ref.shape[1]
    for j in range(0, n, col_chunk):
        o_ref[:, j:j + col_chunk] = _dot(h, w_ref[:, j:j + col_chunk]).astype(o_ref.dtype)


def _norm_matmul(x, g, w, out_dtype, row_tile):
    n, d = x.shape
    ncol = w.shape[1]
    return pl.pallas_call(
        functools.partial(_norm_matmul_kernel, col_chunk=512),
        out_shape=jax.ShapeDtypeStruct((n, ncol), out_dtype),
        grid=(n // row_tile,),
        in_specs=[pl.BlockSpec((row_tile, d), lambda i: (i, 0)),
                  _full((1, d)),
                  _full((d, ncol))],
        out_specs=pl.BlockSpec((row_tile, ncol), lambda i: (i, 0)),
        compiler_params=_params(1),
        name="norm_matmul",
    )(x, g, w)


def _mixer_kernel(p_ref, raw_ref, hgw_ref, cvw_ref, cvb_ref, lnw_ref, lnb_ref, plw_ref, pls_ref,
                  lcw_ref, lcb_ref, lruw_ref, lrub_ref, lam_ref, ones_ref,
                  out_ref,
                  st_ref, zbuf, ubuf, xbuf, hlast_ref,
                  q_sc, k_sc, v_sc, b_sc, o_sc, p_sc, r_sc, a_sc, bt_sc, hs_sc,
                  *, layer):
    T = p_ref.shape[0]
    W = BRANCH_W
    t_idx = pl.program_id(1)

    @pl.when(t_idx == 0)
    def _reset():
        st_ref[...] = jnp.zeros_like(st_ref)
        zbuf[0:CV_HALO, :] = jnp.zeros((CV_HALO, W), F32)
        ubuf[0:POOL_HALO, :] = jnp.zeros((POOL_HALO, W), F32)
        xbuf[0:LRU_HALO, :] = jnp.zeros((LRU_HALO, W), F32)
        hlast_ref[...] = jnp.zeros_like(hlast_ref)

    def col(i):
        return p_ref[:, i * W:(i + 1) * W].astype(F32)

    raw = raw_ref[...]
    depth = raw.shape[0]
    rows = [raw[i:i + 1, :] for i in range(depth)]
    mx = functools.reduce(jnp.maximum, rows)
    es = [jnp.exp(r - mx) for r in rows]
    tot = functools.reduce(lambda a, b: a + b, es)
    lb = jnp.zeros_like(mx)
    for i in range(1, layer + 1):
        lb = lb + es[i] / tot

    hq = col(0)
    q_sc[...] = hq * _sigmoid(hq)
    fg = lb + (1.0 - lb) * _sigmoid(col(1))
    k_sc[...] = 1.0 - fg
    v_sc[...] = col(2)
    logf = jnp.log(fg)
    a1 = logf.astype(BF16)
    r1 = logf - a1.astype(F32)
    a2 = r1.astype(BF16)
    a3 = (r1 - a2.astype(F32)).astype(BF16)
    ri = lax.broadcasted_iota(jnp.int32, (T, T), 0)
    ci = lax.broadcasted_iota(jnp.int32, (T, T), 1)
    shift = CHUNK.bit_length() - 1
    same_chunk = jnp.right_shift(ri, shift) == jnp.right_shift(ci, shift)
    tri = jnp.where((ri >= ci) & same_chunk, 1.0, 0.0).astype(BF16)
    b_sc[...] = _dot(tri, a1) + _dot(tri, a2) + _dot(tri, a3)

    ones2 = ones_ref[...]
    n_sub = CHUNK // SUB
    lane64 = lax.broadcasted_iota(jnp.int32, (SUB, CHUNK), 1)
    row_sub = lax.broadcasted_iota(jnp.int32, (SUB, W), 0)

    def chunk_body(c, carry):
        r0 = pl.multiple_of(c * CHUNK, CHUNK)
        b = b_sc[pl.ds(r0, CHUNK), :]
        q = q_sc[pl.ds(r0, CHUNK), :]
        kk = k_sc[pl.ds(r0, CHUNK), :]
        v = v_sc[pl.ds(r0, CHUNK), :]
        vb = v.astype(BF16)
        b_last = b_sc[pl.ds(r0 + CHUNK - 1, 1), :]
        e_last = jnp.exp(b_last)
        qe = (q * jnp.exp(b)).astype(BF16)
        kd = kk * jnp.exp(b_last - b)
        q_parts, k_parts = [], []
        for j in range(n_sub - 1):
            m = b_sc[pl.ds(r0 + SUB * (j + 1) - 1, 1), :]
            lo = SUB * (j + 1)
            q_parts.append(q[lo:, :] * jnp.exp(b[lo:, :] - m))
            k_parts.append(kk[SUB * j:lo, :] * jnp.exp(m - b[SUB * j:lo, :]))
        k_parts.append(jnp.zeros((SUB, W), F32))
        q_stack = jnp.concatenate(q_parts, axis=0).astype(BF16)
        k_tilde = jnp.concatenate(k_parts, axis=0).astype(BF16)

        o_parts = []
        for h in range(HG_HEADS):
            sl = slice(h * HG_DK, (h + 1) * HG_DK)
            st = st_ref[h]
            o_h = _dot_nt(qe[:, sl], st.astype(BF16))
            a_st = _dot_nt(q_stack[:, sl], k_tilde[:, sl])
            att_rows = [jnp.zeros((SUB, CHUNK), F32)]
            offs = [0, 48, 80]
            for i in range(1, n_sub):
                acc = jnp.zeros((SUB, CHUNK), F32)
                for j in range(i):
                    piece = a_st[offs[j] + SUB * (i - j - 1): offs[j] + SUB * (i - j), :]
                    acc = acc + jnp.where((lane64 >= SUB * j) & (lane64 < SUB * (j + 1)), piece, 0.0)
                att_rows.append(acc)
            att = jnp.concatenate(att_rows, axis=0).astype(BF16)
            o_h = o_h + _dot(att, vb[:, sl])
            o_parts.append(o_h)
            vt = v[:, sl].T.astype(BF16)
            st_ref[h] = st * e_last[:, sl] + _dot(vt, kd[:, sl].astype(BF16))
        o = jnp.concatenate(o_parts, axis=1)

        for blk in range(n_sub):
            base = SUB * blk
            qb = q[base:base + SUB, :]
            bb = b[base:base + SUB, :]
            for s in range(SUB):
                bs = b_sc[pl.ds(r0 + base + s, 1), :]
                ks = k_sc[pl.ds(r0 + base + s, 1), :]
                p = jnp.where(row_sub >= s, qb * ks * jnp.exp(bb - bs), 0.0)
                p_sc[(base + s) * SUB:(base + s + 1) * SUB, :] = p.astype(BF16)
        for half in range(2):
            hs = slice(half * 256, (half + 1) * 256)
            r_sc[:, hs] = _dot(p_sc[:, hs], ones2)
        od_parts = []
        for blk in range(n_sub):
            base = SUB * blk
            acc = jnp.zeros((SUB, W), F32)
            for s in range(SUB):
                vs = v_sc[pl.ds(r0 + base + s, 1), :]
                acc = acc + r_sc[(base + s) * SUB:(base + s + 1) * SUB, :] * vs
            od_parts.append(acc)
        o_sc[pl.ds(r0, CHUNK), :] = o + jnp.concatenate(od_parts, axis=0)
        return carry

    lax.fori_loop(0, T // CHUNK, chunk_body, 0)

    o = o_sc[...]
    osq = (o * o).astype(BF16)
    ssq = jnp.concatenate([_dot(osq[:, 0:256], ones2), _dot(osq[:, 256:512], ones2)], axis=1)
    hg = col(3)
    o = o * lax.rsqrt(ssq * (1.0 / HG_DK) + EPS) * hgw_ref[...]
    out_ref[:, 0:W] = (o * (hg * _sigmoid(hg))).astype(out_ref.dtype)

    z = col(4) * _sigmoid(col(5))
    zbuf[CV_HALO:CV_HALO + T, :] = z
    acc = jnp.zeros((T, W), F32) + cvb_ref[...]
    for k in range(CV_KERNEL):
        off = CV_HALO - (CV_KERNEL - 1) + k
        acc = acc + cvw_ref[k:k + 1, :] * zbuf[off:off + T, :]
    zbuf[0:CV_HALO, :] = zbuf[T:T + CV_HALO, :]
    mu = jnp.mean(acc, axis=-1, keepdims=True)
    dlt = acc - mu
    var = jnp.mean(dlt * dlt, axis=-1, keepdims=True)
    y = dlt * lax.rsqrt(var + EPS) * lnw_ref[...] + lnb_ref[...]
    out_ref[:, W:2 * W] = (y * _sigmoid(y)).astype(out_ref.dtype)

    u = col(6)
    ubuf[POOL_HALO:POOL_HALO + T, :] = u
    pos = t_idx * T + lax.broadcasted_iota(jnp.int32, (T, 1), 0)
    gw = W // len(POOL_WINDOWS)
    pooled = []
    for g, win in enumerate(POOL_WINDOWS):
        sl = slice(g * gw, (g + 1) * gw)
        ssum = ubuf[POOL_HALO:POOL_HALO + T, sl]
        for k in range(1, win):
            ssum = ssum + ubuf[POOL_HALO - k:POOL_HALO - k + T, sl]
        cnt = jnp.minimum(pos + 1, win).astype(F32)
        pooled.append(ssum / cnt - u[:, sl])
    ubuf[0:POOL_HALO, :] = ubuf[T:T + POOL_HALO, :]
    pooled = jnp.concatenate(pooled, axis=1).astype(BF16)
    out_ref[:, 2 * W:3 * W] = (_dot(pooled, plw_ref[...]) * pls_ref[...]).astype(out_ref.dtype)

    xbuf[LRU_HALO:LRU_HALO + T, :] = col(7)
    xc = jnp.zeros((T, W), F32) + lcb_ref[...]
    for k in range(LRU_CONV):
        off = LRU_HALO - (LRU_CONV - 1) + k
        xc = xc + lcw_ref[k:k + 1, :] * xbuf[off:off + T, :]
    xbuf[0:LRU_HALO, :] = xbuf[T:T + LRU_HALO, :]
    gates = _dot(xc.astype(BF16), lruw_ref[...]) + lrub_ref[...]
    r = _sigmoid(gates[:, 0:W])
    ig = _sigmoid(gates[:, W:2 * W])
    nl = -lam_ref[...]
    softplus = jnp.maximum(nl, 0.0) + jnp.log(1.0 + jnp.exp(-jnp.abs(nl)))
    log_a = (-LRU_C) * r * softplus
    a = jnp.exp(log_a)
    a_sc[...] = a
    bt_sc[...] = jnp.sqrt(1.0 - a * a) * (ig * xc)

    def scan_body(i, h):
        h = a_sc[pl.ds(i, 1), :] * h + bt_sc[pl.ds(i, 1), :]
        hs_sc[pl.ds(i, 1), :] = h
        return h

    hlast_ref[...] = lax.fori_loop(0, T, scan_body, hlast_ref[...], unroll=8)
    yb = col(8)
    gelu = 0.5 * yb * (1.0 + jnp.tanh(0.7978845608028654 * (yb + 0.044715 * (yb * yb * yb))))
    out_ref[:, 3 * W:4 * W] = (hs_sc[...] * gelu).astype(out_ref.dtype)


def _mixers(proj, raw, hgw, cvw, cvb, lnw, lnb, plw, pls, lcw, lcb, lruw, lrub, lam, ones2,
            *, layer, seq_tile):
    bsz, seq, pw = proj.shape
    T = seq_tile
    W = BRANCH_W
    consts = [raw, hgw, cvw, cvb, lnw, lnb, plw, pls, lcw, lcb, lruw, lrub, lam, ones2]
    scratch = [
        pltpu.VMEM((HG_HEADS, HG_DK, HG_DK), F32),
        pltpu.VMEM((CV_HALO + T, W), F32),
        pltpu.VMEM((POOL_HALO + T, W), F32),
        pltpu.VMEM((LRU_HALO + T, W), F32),
        pltpu.VMEM((1, W), F32),
        pltpu.VMEM((T, W), F32), pltpu.VMEM((T, W), F32), pltpu.VMEM((T, W), F32),
        pltpu.VMEM((T, W), F32), pltpu.VMEM((T, W), F32),
        pltpu.VMEM((CHUNK * SUB, W), BF16),
        pltpu.VMEM((CHUNK * SUB, W), F32),
        pltpu.VMEM((T, W), F32), pltpu.VMEM((T, W), F32), pltpu.VMEM((T, W), F32),
    ]
    return pl.pallas_call(
        functools.partial(_mixer_kernel, layer=layer),
        out_shape=jax.ShapeDtypeStruct((bsz, seq, N_BRANCH * W), BF16),
        grid=(bsz, seq // T),
        in_specs=[pl.BlockSpec((None, T, pw), lambda b, t: (b, t, 0))]
                 + [_full(c.shape) for c in consts],
        out_specs=pl.BlockSpec((None, T, N_BRANCH * W), lambda b, t: (b, t, 0)),
        scratch_shapes=scratch,
        compiler_params=_params(2),
        name="mixers",
    )(proj, *consts)


def _merge_kernel(x_ref, br_ref, g_ref, wg_ref, gb_ref, wb_ref, wo_ref, o_ref):
    x = x_ref[...]
    d = x.shape[1]
    h = _rms(x, g_ref[...]).astype(BF16)
    merged = jnp.zeros_like(x)
    for k in range(N_BRANCH):
        gate = _sigmoid(_dot(h, wg_ref[:, k * d:(k + 1) * d]) + gb_ref[:, k * d:(k + 1) * d])
        up = _dot(br_ref[:, k * BRANCH_W:(k + 1) * BRANCH_W], wb_ref[k])
        merged = merged + gate * up
    o_ref[...] = x + _dot(merged.astype(BF16), wo_ref[...])


def _merge(x, br, g, wg, gb, wb, wo, row_tile):
    n, d = x.shape
    return pl.pallas_call(
        _merge_kernel,
        out_shape=jax.ShapeDtypeStruct((n, d), x.dtype),
        grid=(n // row_tile,),
        in_specs=[pl.BlockSpec((row_tile, d), lambda i: (i, 0)),
                  pl.BlockSpec((row_tile, N_BRANCH * BRANCH_W), lambda i: (i, 0)),
                  _full((1, d)), _full(wg.shape), _full(gb.shape), _full(wb.shape), _full(wo.shape)],
        out_specs=pl.BlockSpec((row_tile, d), lambda i: (i, 0)),
        compiler_params=_params(1),
        name="merge",
    )(x, br, g, wg, gb, wb, wo)


def _attn_kernel(x_ref, kv_ref, g_ref, wq_ref, wo_ref, o_ref):
    x = x_ref[...]
    d = x.shape[1]
    hd = d // XA_HEADS
    h = _rms(x, g_ref[...]).astype(BF16)
    q = _dot(h, wq_ref[...]).astype(BF16)
    outs = []
    for i in range(XA_HEADS):
        k = kv_ref[:, i * hd:(i + 1) * hd]
        v = kv_ref[:, d + i * hd:d + (i + 1) * hd]
        s = _dot_nt(q[:, i * hd:(i + 1) * hd], k) * (hd ** -0.5)
        e = jnp.exp(s - jnp.max(s, axis=-1, keepdims=True))
        p = e / jnp.sum(e, axis=-1, keepdims=True)
        outs.append(_dot(p.astype(BF16), v))
    o = jnp.concatenate(outs, axis=1).astype(BF16)
    o_ref[...] = x + _dot(o, wo_ref[...])


def _attention(x, kv, g, wq, wo, row_tile):
    bsz, seq, d = x.shape
    n_mem = kv.shape[1]
    return pl.pallas_call(
        _attn_kernel,
        out_shape=jax.ShapeDtypeStruct(x.shape, x.dtype),
        grid=(bsz, seq // row_tile),
        in_specs=[pl.BlockSpec((None, row_tile, d), lambda b, t: (b, t, 0)),
                  pl.BlockSpec((None, n_mem, 2 * d), lambda b, t: (b, 0, 0)),
                  _full((1, d)), _full(wq.shape), _full(wo.shape)],
        out_specs=pl.BlockSpec((None, row_tile, d), lambda b, t: (b, t, 0)),
        compiler_params=_params(2),
        name="mem_attention",
    )(x, kv, g, wq, wo)


def _ffn_kernel(x_ref, g_ref, w1_ref, w2_ref, fg_ref, o_ref, *, final_norm, ff_chunk):
    x = x_ref[...]
    h = _rms(x, g_ref[...]).astype(BF16)
    acc = x
    for j in range(0, w1_ref.shape[1], ff_chunk):
        a = jnp.maximum(_dot(h, w1_ref[:, j:j + ff_chunk]), 0.0)
        acc = acc + _dot((a * a).astype(BF16), w2_ref[j:j + ff_chunk, :])
    if final_norm:
        acc = _rms(acc, fg_ref[...])
    o_ref[...] = acc


def _ffn(x, g, w1, w2, fg, final_norm, row_tile):
    n, d = x.shape
    return pl.pallas_call(
        functools.partial(_ffn_kernel, final_norm=final_norm, ff_chunk=1024),
        out_shape=jax.ShapeDtypeStruct((n, d), x.dtype),
        grid=(n // row_tile,),
        in_specs=[pl.BlockSpec((row_tile, d), lambda i: (i, 0)),
                  _full((1, d)), _full(w1.shape), _full(w2.shape), _full((1, d))],
        out_specs=pl.BlockSpec((row_tile, d), lambda i: (i, 0)),
        compiler_params=_params(1),
        name="ffn",
    )(x, g, w1, w2, fg)


def _block_diag(blocks):
    g, a, b = blocks.shape
    eye = jnp.eye(g, dtype=blocks.dtype)
    return (eye[:, None, :, None] * blocks[:, :, None, :]).reshape(g * a, g * b)


def kernel(x, mem, norm_mix_w, w_in, hg_lb_raw, hg_norm_w, cv_dw_w, cv_dw_b, cv_ln_w, cv_ln_b, pl_w, pl_scale, lru_conv_w, lru_conv_b, lru_wa, lru_ba, lru_wx, lru_bx, lru_lambda, gate_b, w_branch, w_out, norm_mem_w, mem_norm_w, xa_wq, xa_wkv, xa_wo, norm_ffn_w, ffn_w1, ffn_w2, final_norm_w):
    bsz, seq, d = x.shape
    depth = w_in.shape[0]
    n = bsz * seq
    n_mem = mem.shape[1]
    row_tile = min(ROW_TILE, n)
    attn_tile = min(ROW_TILE, seq)
    seq_tile = min(SEQ_TILE, seq)
    assert seq % seq_tile == 0 and seq_tile % CHUNK == 0 and n % row_tile == 0 and seq % attn_tile == 0
    assert w_in.shape[2] == MIX_IN_W + N_BRANCH * d

    ones2 = _block_diag(jnp.ones((2, HG_DK, HG_DK), BF16))
    raw = hg_lb_raw.astype(F32)
    row = lambda a: a.reshape(1, -1).astype(F32)
    mem2 = mem.reshape(bsz * n_mem, d)
    mem_tile = min(ROW_TILE, bsz * n_mem)

    xf = x.reshape(n, d)
    for l in range(depth):
        w_mix = w_in[l, :, :MIX_IN_W].astype(BF16)
        w_gate = w_in[l, :, MIX_IN_W:].astype(BF16)
        proj = _norm_matmul(xf, row(norm_mix_w[l]), w_mix, BF16, row_tile)
        lruw = jnp.concatenate([_block_diag(lru_wa[l]), _block_diag(lru_wx[l])], axis=1).astype(BF16)
        lrub = jnp.concatenate([row(lru_ba[l]), row(lru_bx[l])], axis=1)
        br = _mixers(
            proj.reshape(bsz, seq, MIX_IN_W), raw, row(jnp.tile(hg_norm_w[l], HG_HEADS)),
            cv_dw_w[l].astype(F32), row(cv_dw_b[l]), row(cv_ln_w[l]), row(cv_ln_b[l]),
            _block_diag(pl_w[l]).astype(BF16), row(pl_scale[l]),
            lru_conv_w[l].astype(F32), row(lru_conv_b[l]), lruw, lrub, row(lru_lambda[l]), ones2,
            layer=l, seq_tile=seq_tile)
        xf = _merge(xf, br.reshape(n, N_BRANCH * BRANCH_W), row(norm_mix_w[l]), w_gate,
                    row(gate_b[l]), w_branch[l].astype(BF16), w_out[l].astype(BF16), row_tile)
        kv = _norm_matmul(mem2, row(mem_norm_w[l]), xa_wkv[l].astype(BF16), BF16, mem_tile)
        xf = _attention(xf.reshape(bsz, seq, d), kv.reshape(bsz, n_mem, 2 * d), row(norm_mem_w[l]),
                        xa_wq[l].astype(BF16), xa_wo[l].astype(BF16), attn_tile).reshape(n, d)
        xf = _ffn(xf, row(norm_ffn_w[l]), ffn_w1[l].astype(BF16), ffn_w2[l].astype(BF16),
                  row(final_norm_w), l == depth - 1, row_tile)
    return xf.reshape(bsz, seq, d)
```

```python
import functools

import jax
import jax.numpy as jnp
from jax import lax
from jax.experimental import pallas as pl
from jax.experimental.pallas import tpu as pltpu

F32 = jnp.float32
BF16 = jnp.bfloat16

EPS = 1e-6
CHUNK = 64
SUB = 16
HG_HEADS = 4
HG_DK = 128
BRANCH_W = 512
N_BRANCH = 4
CV_KERNEL = 31
CV_HALO = 32
POOL_WINDOWS = (2, 4, 8, 16)
POOL_HALO = 16
LRU_CONV = 4
LRU_HALO = 8
LRU_C = 8.0
XA_HEADS = 4
LANES = 128
NLT = BRANCH_W // LANES
MIX_IN_W = 9 * BRANCH_W

VMEM_LIMIT_BYTES = 56 * 1024 * 1024

SEQ_TILE = 256
ROW_TILE = 512


def _params(n_axes):
    return pltpu.CompilerParams(dimension_semantics=("arbitrary",) * n_axes,
                                vmem_limit_bytes=VMEM_LIMIT_BYTES)


def _rms(x, w):
    return x * lax.rsqrt(jnp.mean(x * x, axis=-1, keepdims=True) + EPS) * w


def _sigmoid(x):
    return 0.5 * jnp.tanh(0.5 * x) + 0.5


def _sigmoid_exp(x):
    return 1.0 / (1.0 + jnp.exp(-x))


def _dot(a, b):
    return jnp.dot(a, b, preferred_element_type=F32)


def _dot_nt(a, b):
    return lax.dot_general(a, b, (((1,), (1,)), ((), ())), preferred_element_type=F32)


def _full(shape):
    return pl.BlockSpec(shape, lambda *_: (0,) * len(shape))


def _norm_matmul_kernel(x_ref, g_ref, w_ref, o_ref, *, col_chunk):
    h = _rms(x_ref[...], g_ref[...]).astype(BF16)
    n = w_ref.shape[1]
    for j in range(0, n, col_chunk):
        o_ref[:, j:j + col_chunk] = _dot(h, w_ref[:, j:j + col_chunk]).astype(o_ref.dtype)


def _norm_matmul(x, g, w, out_dtype, row_tile):
    n, d = x.shape
    ncol = w.shape[1]
    return pl.pallas_call(
        functools.partial(_norm_matmul_kernel, col_chunk=512),
        out_shape=jax.ShapeDtypeStruct((n, ncol), out_dtype),
        grid=(n // row_tile,),
        in_specs=[pl.BlockSpec((row_tile, d), lambda i: (i, 0)),
                  _full((1, d)),
                  _full((d, ncol))],
        out_specs=pl.BlockSpec((row_tile, ncol), lambda i: (i, 0)),
        compiler_params=_params(1),
        name="norm_matmul",
    )(x, g, w)


def _mixer_kernel(p_ref, raw_ref, hgw_ref, cvw_ref, cvb_ref, lnw_ref, lnb_ref, plw_ref, pls_ref,
                  lcw_ref, lcb_ref, lruw_ref, lrub_ref, lam_ref, ones_ref,
                  out_ref,
                  st_ref, zbuf, ubuf, xbuf, hlast_ref,
                  q_sc, k_sc, v_sc, b_sc, o_sc, p_sc, r_sc, a_sc, bt_sc, hs_sc,
                  *, layer):
    T = p_ref.shape[0]
    W = BRANCH_W
    t_idx = pl.program_id(1)

    @pl.when(t_idx == 0)
    def _reset():
        st_ref[...] = jnp.zeros_like(st_ref)
        zbuf[:, 0:CV_HALO, :] = jnp.zeros((NLT, CV_HALO, LANES), F32)
        ubuf[:, 0:POOL_HALO, :] = jnp.zeros((NLT, POOL_HALO, LANES), F32)
        xbuf[:, 0:LRU_HALO, :] = jnp.zeros((NLT, LRU_HALO, LANES), F32)
        hlast_ref[...] = jnp.zeros_like(hlast_ref)

    def col(i):
        return p_ref[:, i * W:(i + 1) * W].astype(F32)

    raw = raw_ref[...]
    depth = raw.shape[0]
    rows = [raw[i:i + 1, :] for i in range(depth)]
    mx = functools.reduce(jnp.maximum, rows)
    es = [jnp.exp(r - mx) for r in rows]
    tot = functools.reduce(lambda a, b: a + b, es)
    lb = jnp.zeros_like(mx)
    for i in range(1, layer + 1):
        lb = lb + es[i] / tot

    hq = col(0)
    q_sc[...] = hq * _sigmoid(hq)
    fg = lb + (1.0 - lb) * _sigmoid_exp(col(1))
    k_sc[...] = 1.0 - fg
    v_sc[...] = col(2)
    logf = jnp.log2(fg)
    a1 = logf.astype(BF16)
    r1 = logf - a1.astype(F32)
    a2 = r1.astype(BF16)
    a3 = (r1 - a2.astype(F32)).astype(BF16)
    ri = lax.broadcasted_iota(jnp.int32, (T, T), 0)
    ci = lax.broadcasted_iota(jnp.int32, (T, T), 1)
    shift = CHUNK.bit_length() - 1
    same_chunk = jnp.right_shift(ri, shift) == jnp.right_shift(ci, shift)
    tri = jnp.where((ri >= ci) & same_chunk, 1.0, 0.0).astype(BF16)
    b_sc[...] = _dot(tri, a1) + _dot(tri, a2) + _dot(tri, a3)

    ones2 = ones_ref[...]
    n_sub = CHUNK // SUB
    lane64 = lax.broadcasted_iota(jnp.int32, (SUB, CHUNK), 1)
    row_sub = lax.broadcasted_iota(jnp.int32, (SUB, W), 0)

    def chunk_body(c, carry):
        r0 = pl.multiple_of(c * CHUNK, CHUNK)
        b = b_sc[pl.ds(r0, CHUNK), :]
        q = q_sc[pl.ds(r0, CHUNK), :]
        kk = k_sc[pl.ds(r0, CHUNK), :]
        v = v_sc[pl.ds(r0, CHUNK), :]
        vb = v.astype(BF16)
        b_last = b_sc[pl.ds(r0 + CHUNK - 1, 1), :]
        e_last = jnp.exp2(b_last)
        qe = (q * jnp.exp2(b)).astype(BF16)
        kd = kk * jnp.exp2(b_last - b)
        q_parts, k_parts = [], []
        for j in range(n_sub - 1):
            m = b_sc[pl.ds(r0 + SUB * (j + 1) - 1, 1), :]
            lo = SUB * (j + 1)
            q_parts.append(q[lo:, :] * jnp.exp2(b[lo:, :] - m))
            k_parts.append(kk[SUB * j:lo, :] * jnp.exp2(m - b[SUB * j:lo, :]))
        k_parts.append(jnp.zeros((SUB, W), F32))
        q_stack = jnp.concatenate(q_parts, axis=0).astype(BF16)
        k_tilde = jnp.concatenate(k_parts, axis=0).astype(BF16)

        o_parts = []
        for h in range(HG_HEADS):
            sl = slice(h * HG_DK, (h + 1) * HG_DK)
            st = st_ref[h]
            o_h = _dot_nt(qe[:, sl], st.astype(BF16))
            a_st = _dot_nt(q_stack[:, sl], k_tilde[:, sl])
            att_rows = [jnp.zeros((SUB, CHUNK), F32)]
            offs = [0, 48, 80]
            for i in range(1, n_sub):
                acc = jnp.zeros((SUB, CHUNK), F32)
                for j in range(i):
                    piece = a_st[offs[j] + SUB * (i - j - 1): offs[j] + SUB * (i - j), :]
                    acc = acc + jnp.where((lane64 >= SUB * j) & (lane64 < SUB * (j + 1)), piece, 0.0)
                att_rows.append(acc)
            att = jnp.concatenate(att_rows, axis=0).astype(BF16)
            o_h = o_h + _dot(att, vb[:, sl])
            o_parts.append(o_h)
            vt = v[:, sl].T.astype(BF16)
            st_ref[h] = st * e_last[:, sl] + _dot(vt, kd[:, sl].astype(BF16))
        o = jnp.concatenate(o_parts, axis=1)

        for blk in range(n_sub):
            base = SUB * blk
            qb = q[base:base + SUB, :]
            bb = b[base:base + SUB, :]
            for s in range(SUB):
                bs = b_sc[pl.ds(r0 + base + s, 1), :]
                ks = k_sc[pl.ds(r0 + base + s, 1), :]
                p = jnp.where(row_sub >= s, qb * ks * jnp.exp2(bb - bs), 0.0)
                p_sc[(base + s) * SUB:(base + s + 1) * SUB, :] = p.astype(BF16)
        for half in range(2):
            hs = slice(half * 256, (half + 1) * 256)
            r_sc[:, hs] = _dot(p_sc[:, hs], ones2)
        od_parts = []
        for blk in range(n_sub):
            base = SUB * blk
            acc = jnp.zeros((SUB, W), F32)
            for s in range(SUB):
                vs = v_sc[pl.ds(r0 + base + s, 1), :]
                acc = acc + r_sc[(base + s) * SUB:(base + s + 1) * SUB, :] * vs
            od_parts.append(acc)
        o_sc[pl.ds(r0, CHUNK), :] = o + jnp.concatenate(od_parts, axis=0)
        return carry

    lax.fori_loop(0, T // CHUNK, chunk_body, 0)

    o = o_sc[...]
    osq = (o * o).astype(BF16)
    ssq = jnp.concatenate([_dot(osq[:, 0:256], ones2), _dot(osq[:, 256:512], ones2)], axis=1)
    hg = col(3)
    o = o * lax.rsqrt(ssq * (1.0 / HG_DK) + EPS) * hgw_ref[...]
    out_ref[:, 0:W] = (o * (hg * _sigmoid(hg))).astype(out_ref.dtype)

    z = col(4) * _sigmoid(col(5))
    conv_parts = []
    for j in range(NLT):
        ls = slice(j * LANES, (j + 1) * LANES)
        zbuf[j, CV_HALO:CV_HALO + T, :] = z[:, ls]
        acc = jnp.zeros((T, LANES), F32) + cvb_ref[:, ls]
        for k in range(CV_KERNEL):
            off = CV_HALO - (CV_KERNEL - 1) + k
            acc = acc + cvw_ref[k:k + 1, ls] * zbuf[j, off:off + T, :]
        zbuf[j, 0:CV_HALO, :] = zbuf[j, T:T + CV_HALO, :]
        conv_parts.append(acc)
    acc = jnp.concatenate(conv_parts, axis=1)
    mu = jnp.mean(acc, axis=-1, keepdims=True)
    dlt = acc - mu
    var = jnp.mean(dlt * dlt, axis=-1, keepdims=True)
    y = dlt * lax.rsqrt(var + EPS) * lnw_ref[...] + lnb_ref[...]
    out_ref[:, W:2 * W] = (y * _sigmoid(y)).astype(out_ref.dtype)

    u = col(6)
    pos = t_idx * T + lax.broadcasted_iota(jnp.int32, (T, 1), 0)
    gw = W // len(POOL_WINDOWS)
    pooled = []
    for g, win in enumerate(POOL_WINDOWS):
        sl = slice(g * gw, (g + 1) * gw)
        ubuf[g, POOL_HALO:POOL_HALO + T, :] = u[:, sl]
        ssum = u[:, sl]
        for k in range(1, win):
            ssum = ssum + ubuf[g, POOL_HALO - k:POOL_HALO - k + T, :]
        ubuf[g, 0:POOL_HALO, :] = ubuf[g, T:T + POOL_HALO, :]
        cnt = jnp.minimum(pos + 1, win).astype(F32)
        pooled.append(ssum / cnt - u[:, sl])
    pooled = jnp.concatenate(pooled, axis=1).astype(BF16)
    out_ref[:, 2 * W:3 * W] = (_dot(pooled, plw_ref[...]) * pls_ref[...]).astype(out_ref.dtype)

    lru_x = col(7)
    xc_parts = []
    for j in range(NLT):
        ls = slice(j * LANES, (j + 1) * LANES)
        xbuf[j, LRU_HALO:LRU_HALO + T, :] = lru_x[:, ls]
        part = jnp.zeros((T, LANES), F32) + lcb_ref[:, ls]
        for k in range(LRU_CONV):
            off = LRU_HALO - (LRU_CONV - 1) + k
            part = part + lcw_ref[k:k + 1, ls] * xbuf[j, off:off + T, :]
        xbuf[j, 0:LRU_HALO, :] = xbuf[j, T:T + LRU_HALO, :]
        xc_parts.append(part)
    xc = jnp.concatenate(xc_parts, axis=1)
    gates = _dot(xc.astype(BF16), lruw_ref[...]) + lrub_ref[...]
    r = _sigmoid(gates[:, 0:W])
    ig = _sigmoid(gates[:, W:2 * W])
    nl = -lam_ref[...]
    softplus = jnp.maximum(nl, 0.0) + jnp.log(1.0 + jnp.exp(-jnp.abs(nl)))
    log_a = (-LRU_C) * r * softplus
    a = jnp.exp(log_a)
    a_sc[...] = a
    bt_sc[...] = jnp.sqrt(1.0 - a * a) * (ig * xc)

    def scan_body(i, h):
        h = a_sc[pl.ds(i, 1), :] * h + bt_sc[pl.ds(i, 1), :]
        hs_sc[pl.ds(i, 1), :] = h
        return h

    hlast_ref[...] = lax.fori_loop(0, T, scan_body, hlast_ref[...], unroll=8)
    yb = col(8)
    gelu = 0.5 * yb * (1.0 + jnp.tanh(0.7978845608028654 * (yb + 0.044715 * (yb * yb * yb))))
    out_ref[:, 3 * W:4 * W] = (hs_sc[...] * gelu).astype(out_ref.dtype)


def _mixers(proj, raw, hgw, cvw, cvb, lnw, lnb, plw, pls, lcw, lcb, lruw, lrub, lam, ones2,
            *, layer, seq_tile):
    bsz, seq, pw = proj.shape
    T = seq_tile
    W = BRANCH_W
    consts = [raw, hgw, cvw, cvb, lnw, lnb, plw, pls, lcw, lcb, lruw, lrub, lam, ones2]
    scratch = [
        pltpu.VMEM((HG_HEADS, HG_DK, HG_DK), F32),
        pltpu.VMEM((NLT, CV_HALO + T, LANES), F32),
        pltpu.VMEM((NLT, POOL_HALO + T, LANES), F32),
        pltpu.VMEM((NLT, LRU_HALO + T, LANES), F32),
        pltpu.VMEM((1, W), F32),
        pltpu.VMEM((T, W), F32), pltpu.VMEM((T, W), F32), pltpu.VMEM((T, W), F32),
        pltpu.VMEM((T, W), F32), pltpu.VMEM((T, W), F32),
        pltpu.VMEM((CHUNK * SUB, W), BF16),
        pltpu.VMEM((CHUNK * SUB, W), F32),
        pltpu.VMEM((T, W), F32), pltpu.VMEM((T, W), F32), pltpu.VMEM((T, W), F32),
    ]
    return pl.pallas_call(
        functools.partial(_mixer_kernel, layer=layer),
        out_shape=jax.ShapeDtypeStruct((bsz, seq, N_BRANCH * W), BF16),
        grid=(bsz, seq // T),
        in_specs=[pl.BlockSpec((None, T, pw), lambda b, t: (b, t, 0))]
                 + [_full(c.shape) for c in consts],
        out_specs=pl.BlockSpec((None, T, N_BRANCH * W), lambda b, t: (b, t, 0)),
        scratch_shapes=scratch,
        compiler_params=_params(2),
        name="mixers",
    )(proj, *consts)


def _merge_kernel(x_ref, br_ref, g_ref, wg_ref, gb_ref, wb_ref, wo_ref, o_ref):
    x = x_ref[...]
    d = x.shape[1]
    h = _rms(x, g_ref[...]).astype(BF16)
    merged = jnp.zeros_like(x)
    for k in range(N_BRANCH):
        gate = _sigmoid(_dot(h, wg_ref[:, k * d:(k + 1) * d]) + gb_ref[:, k * d:(k + 1) * d])
        up = _dot(br_ref[:, k * BRANCH_W:(k + 1) * BRANCH_W], wb_ref[k])
        merged = merged + gate * up
    o_ref[...] = x + _dot(merged.astype(BF16), wo_ref[...])


def _merge(x, br, g, wg, gb, wb, wo, row_tile):
    n, d = x.shape
    return pl.pallas_call(
        _merge_kernel,
        out_shape=jax.ShapeDtypeStruct((n, d), x.dtype),
        grid=(n // row_tile,),
        in_specs=[pl.BlockSpec((row_tile, d), lambda i: (i, 0)),
                  pl.BlockSpec((row_tile, N_BRANCH * BRANCH_W), lambda i: (i, 0)),
                  _full((1, d)), _full(wg.shape), _full(gb.shape), _full(wb.shape), _full(wo.shape)],
        out_specs=pl.BlockSpec((row_tile, d), lambda i: (i, 0)),
        compiler_params=_params(1),
        name="merge",
    )(x, br, g, wg, gb, wb, wo)


def _attn_kernel(x_ref, kv_ref, g_ref, wq_ref, wo_ref, o_ref):
    x = x_ref[...]
    d = x.shape[1]
    hd = d // XA_HEADS
    h = _rms(x, g_ref[...]).astype(BF16)
    q = _dot(h, wq_ref[...]).astype(BF16)
    outs = []
    for i in range(XA_HEADS):
        k = kv_ref[:, i * hd:(i + 1) * hd]
        v = kv_ref[:, d + i * hd:d + (i + 1) * hd]
        s = _dot_nt(q[:, i * hd:(i + 1) * hd], k) * (hd ** -0.5)
        e = jnp.exp(s - jnp.max(s, axis=-1, keepdims=True))
        p = e / jnp.sum(e, axis=-1, keepdims=True)
        outs.append(_dot(p.astype(BF16), v))
    o = jnp.concatenate(outs, axis=1).astype(BF16)
    o_ref[...] = x + _dot(o, wo_ref[...])


def _attention(x, kv, g, wq, wo, row_tile):
    bsz, seq, d = x.shape
    n_mem = kv.shape[1]
    return pl.pallas_call(
        _attn_kernel,
        out_shape=jax.ShapeDtypeStruct(x.shape, x.dtype),
        grid=(bsz, seq // row_tile),
        in_specs=[pl.BlockSpec((None, row_tile, d), lambda b, t: (b, t, 0)),
                  pl.BlockSpec((None, n_mem, 2 * d), lambda b, t: (b, 0, 0)),
                  _full((1, d)), _full(wq.shape), _full(wo.shape)],
        out_specs=pl.BlockSpec((None, row_tile, d), lambda b, t: (b, t, 0)),
        compiler_params=_params(2),
        name="mem_attention",
    )(x, kv, g, wq, wo)


def _ffn_kernel(x_ref, g_ref, w1_ref, w2_ref, fg_ref, o_ref, *, final_norm, ff_chunk):
    x = x_ref[...]
    h = _rms(x, g_ref[...]).astype(BF16)
    acc = x
    for j in range(0, w1_ref.shape[1], ff_chunk):
        a = jnp.maximum(_dot(h, w1_ref[:, j:j + ff_chunk]), 0.0)
        acc = acc + _dot((a * a).astype(BF16), w2_ref[j:j + ff_chunk, :])
    if final_norm:
        acc = _rms(acc, fg_ref[...])
    o_ref[...] = acc


def _ffn(x, g, w1, w2, fg, final_norm, row_tile):
    n, d = x.shape
    return pl.pallas_call(
        functools.partial(_ffn_kernel, final_norm=final_norm, ff_chunk=1024),
        out_shape=jax.ShapeDtypeStruct((n, d), x.dtype),
        grid=(n // row_tile,),
        in_specs=[pl.BlockSpec((row_tile, d), lambda i: (i, 0)),
                  _full((1, d)), _full(w1.shape), _full(w2.shape), _full((1, d))],
        out_specs=pl.BlockSpec((row_tile, d), lambda i: (i, 0)),
        compiler_params=_params(1),
        name="ffn",
    )(x, g, w1, w2, fg)


def _block_diag(blocks):
    g, a, b = blocks.shape
    eye = jnp.eye(g, dtype=blocks.dtype)
    return (eye[:, None, :, None] * blocks[:, :, None, :]).reshape(g * a, g * b)


def kernel(x, mem, norm_mix_w, w_in, hg_lb_raw, hg_norm_w, cv_dw_w, cv_dw_b, cv_ln_w, cv_ln_b, pl_w, pl_scale, lru_conv_w, lru_conv_b, lru_wa, lru_ba, lru_wx, lru_bx, lru_lambda, gate_b, w_branch, w_out, norm_mem_w, mem_norm_w, xa_wq, xa_wkv, xa_wo, norm_ffn_w, ffn_w1, ffn_w2, final_norm_w):
    bsz, seq, d = x.shape
    depth = w_in.shape[0]
    n = bsz * seq
    n_mem = mem.shape[1]
    row_tile = min(ROW_TILE, n)
    attn_tile = min(ROW_TILE, seq)
    seq_tile = min(SEQ_TILE, seq)
    assert seq % seq_tile == 0 and seq_tile % CHUNK == 0 and n % row_tile == 0 and seq % attn_tile == 0
    assert w_in.shape[2] == MIX_IN_W + N_BRANCH * d

    ones2 = _block_diag(jnp.ones((2, HG_DK, HG_DK), BF16))
    raw = hg_lb_raw.astype(F32)
    row = lambda a: a.reshape(1, -1).astype(F32)
    mem2 = mem.reshape(bsz * n_mem, d)
    mem_tile = min(ROW_TILE, bsz * n_mem)

    xf = x.reshape(n, d)
    for l in range(depth):
        w_mix = w_in[l, :, :MIX_IN_W].astype(BF16)
        w_gate = w_in[l, :, MIX_IN_W:].astype(BF16)
        proj = _norm_matmul(xf, row(norm_mix_w[l]), w_mix, BF16, row_tile)
        lruw = jnp.concatenate([_block_diag(lru_wa[l]), _block_diag(lru_wx[l])], axis=1).astype(BF16)
        lrub = jnp.concatenate([row(lru_ba[l]), row(lru_bx[l])], axis=1)
        br = _mixers(
            proj.reshape(bsz, seq, MIX_IN_W), raw, row(jnp.tile(hg_norm_w[l], HG_HEADS)),
            cv_dw_w[l].astype(F32), row(cv_dw_b[l]), row(cv_ln_w[l]), row(cv_ln_b[l]),
            _block_diag(pl_w[l]).astype(BF16), row(pl_scale[l]),
            lru_conv_w[l].astype(F32), row(lru_conv_b[l]), lruw, lrub, row(lru_lambda[l]), ones2,
            layer=l, seq_tile=seq_tile)
        xf = _merge(xf, br.reshape(n, N_BRANCH * BRANCH_W), row(norm_mix_w[l]), w_gate,
                    row(gate_b[l]), w_branch[l].astype(BF16), w_out[l].astype(BF16), row_tile)
        kv = _norm_matmul(mem2, row(mem_norm_w[l]), xa_wkv[l].astype(BF16), BF16, mem_tile)
        xf = _attention(xf.reshape(bsz, seq, d), kv.reshape(bsz, n_mem, 2 * d), row(norm_mem_w[l]),
                        xa_wq[l].astype(BF16), xa_wo[l].astype(BF16), attn_tile).reshape(n, d)
        xf = _ffn(xf, row(norm_ffn_w[l]), ffn_w1[l].astype(BF16), ffn_w2[l].astype(BF16),
                  row(final_norm_w), l == depth - 1, row_tile)
    return xf.reshape(bsz, seq, d)
```
